```python
import jax, jax.numpy as jnp
from jax import lax
import numpy as np

D_MODEL = 2048
BATCH = 2
SEQ = 4096
DEPTH = 1

D_MIX = D_MODEL
D_ATT = D_MIX // 2
ATT_HEAD_DIM = 128
ATT_HEADS = D_ATT // ATT_HEAD_DIM
ROPE_THETA = 10000.0
MOBA_BLOCK = 256
MOBA_TOPK = 3
MOBA_Q_CHUNK = 32
D_RWKV = D_MIX - D_ATT
RWKV_HEAD_DIM = 64
RWKV_HEADS = D_RWKV // RWKV_HEAD_DIM
D_DECAY_LORA = 64
D_AAA_LORA = 64
D_GATE_LORA = 160
D_RWKV_IN = 3 * D_RWKV + D_DECAY_LORA + D_AAA_LORA + D_GATE_LORA
D_IN = 3 * D_ATT + D_RWKV_IN
N_EXPERTS = 32
TOP_K = 4
D_EXPERT = D_MODEL
SWIGLU_LIMIT = 7.0
SWIGLU_ALPHA = 1.702
MOE_BLOCK = 128
LN_EPS = 1e-5
GN_EPS = 64e-5
NEG_INF = -1e30
DEEPNORM_ALPHA = (2 * DEPTH) ** 0.25
DEEPNORM_BETA = (8 * DEPTH) ** -0.25

kernel_name = 'hybrid_moba_rwkv7_moe_deepnorm'


def _layer_norm(x, g, b):
    xf = x.astype(jnp.float32)
    mu = xf.mean(-1, keepdims=True)
    var = jnp.square(xf - mu).mean(-1, keepdims=True)
    return ((xf - mu) * lax.rsqrt(var + LN_EPS) * g + b).astype(x.dtype)


def _rope(t):
    S, Dh = t.shape[2], t.shape[3]
    half = Dh // 2
    inv = ROPE_THETA ** (-jnp.arange(0, Dh, 2, dtype=jnp.float32) / Dh)
    ang = jnp.arange(S, dtype=jnp.float32)[:, None] * inv[None, :]
    cos = jnp.concatenate([jnp.cos(ang), jnp.cos(ang)], axis=-1)
    sin = jnp.concatenate([jnp.sin(ang), jnp.sin(ang)], axis=-1)
    tf = t.astype(jnp.float32)
    rot = jnp.concatenate([-tf[..., half:], tf[..., :half]], axis=-1)
    return (tf * cos + rot * sin).astype(t.dtype)


def _moba_attention(q, k, v):
    B, H, S, Dh = q.shape
    nb = -(-S // MOBA_BLOCK)
    pad = nb * MOBA_BLOCK - S
    kp = jnp.pad(k, ((0, 0), (0, 0), (0, pad), (0, 0)))
    vp = jnp.pad(v, ((0, 0), (0, 0), (0, pad), (0, 0)))
    kb = kp.reshape(B, H, nb, MOBA_BLOCK, Dh)
    vb = vp.reshape(B, H, nb, MOBA_BLOCK, Dh)
    k_mean = kb.astype(jnp.float32).mean(axis=3)
    n_sel = min(MOBA_TOPK, nb)
    n_chunks = S // MOBA_Q_CHUNK
    q_chunks = jnp.moveaxis(q.reshape(B, H, n_chunks, MOBA_Q_CHUNK, Dh), 2, 0)
    b_idx = jnp.arange(B)[:, None, None, None]
    h_idx = jnp.arange(H)[None, :, None, None]
    blk_ids = jnp.arange(nb)
    scale = Dh ** -0.5

    def one_chunk(args):
        q_blk, ci = args
        q0 = ci * MOBA_Q_CHUNK
        cur = q0 // MOBA_BLOCK
        q_pos = q0 + jnp.arange(MOBA_Q_CHUNK)
        gate = jnp.einsum('bhqd,bhnd->bhqn', q_blk.astype(jnp.float32), k_mean)
        gate = jnp.where(blk_ids < cur, gate, NEG_INF)
        _, sel = lax.top_k(gate, n_sel)
        sel_ok = sel < cur
        k_sel = kb[b_idx, h_idx, sel]
        v_sel = vb[b_idx, h_idx, sel]
        s_sel = jnp.einsum('bhqd,bhqnkd->bhqnk', q_blk, k_sel).astype(jnp.float32) * scale
        s_sel = jnp.where(sel_ok[..., None], s_sel, NEG_INF)
        k_own = lax.dynamic_slice_in_dim(kp, cur * MOBA_BLOCK, MOBA_BLOCK, axis=2)
        v_own = lax.dynamic_slice_in_dim(vp, cur * MOBA_BLOCK, MOBA_BLOCK, axis=2)
        s_own = jnp.einsum('bhqd,bhkd->bhqk', q_blk, k_own).astype(jnp.float32) * scale
        k_pos = cur * MOBA_BLOCK + jnp.arange(MOBA_BLOCK)
        s_own = jnp.where(k_pos[None, :] <= q_pos[:, None], s_own, NEG_INF)
        s_all = jnp.concatenate([s_sel.reshape(B, H, MOBA_Q_CHUNK, n_sel * MOBA_BLOCK), s_own], axis=-1)
        p = jax.nn.softmax(s_all, axis=-1).astype(v.dtype)
        p_sel = p[..., :n_sel * MOBA_BLOCK].reshape(B, H, MOBA_Q_CHUNK, n_sel, MOBA_BLOCK)
        p_own = p[..., n_sel * MOBA_BLOCK:]
        return (jnp.einsum('bhqnk,bhqnkd->bhqd', p_sel, v_sel)
                + jnp.einsum('bhqk,bhkd->bhqd', p_own, v_own))

    out = lax.map(one_chunk, (q_chunks, jnp.arange(n_chunks)))
    return out.transpose(1, 0, 3, 2, 4).reshape(B, S, H * Dh)


def _rwkv7_scan(r, w, k, v, a, b):
    B, T, H, N = r.shape

    def step(state, inp):
        r_t, w_t, k_t, v_t, a_t, b_t = inp
        sa = jnp.einsum('bhij,bhj->bhi', state, a_t)
        state = (state * w_t[:, :, None, :] + sa[..., None] * b_t[:, :, None, :]
                 + v_t[..., None] * k_t[:, :, None, :])
        return state, jnp.einsum('bhij,bhj->bhi', state, r_t)

    xs = tuple(jnp.moveaxis(t, 1, 0) for t in (r, w, k, v, a, b))
    _, ys = lax.scan(step, jnp.zeros((B, H, N, N), jnp.float32), xs)
    return jnp.moveaxis(ys, 0, 1)


def _rwkv7_mixer(p_r, p_k, p_v, p_w, p_a, p_g, w0, w_up, a0, a_up, g_up, k_k, k_a, r_k, lnx_g, lnx_b):
    B, T, _ = p_r.shape
    H, N = RWKV_HEADS, RWKV_HEAD_DIM
    f32 = jnp.float32
    w = -jax.nn.softplus(-(w0 + jnp.tanh(p_w) @ w_up).astype(f32)) - 0.5
    decay = jnp.exp(-jnp.exp(w))
    a = jax.nn.sigmoid((a0 + p_a @ a_up).astype(f32))
    g = jax.nn.sigmoid(p_g) @ g_up
    kk = (p_k * k_k).astype(f32).reshape(B, T, H, N)
    kk = kk / jnp.maximum(jnp.sqrt(jnp.sum(kk * kk, axis=-1, keepdims=True)), 1e-12)
    k = p_k.astype(f32) * (1.0 + (a - 1.0) * k_a)
    r4 = p_r.astype(f32).reshape(B, T, H, N)
    k4 = k.reshape(B, T, H, N)
    v4 = p_v.astype(f32).reshape(B, T, H, N)
    a4 = a.reshape(B, T, H, N)
    y = _rwkv7_scan(r4, decay.reshape(B, T, H, N), k4, v4, -kk, kk * a4)
    mu = y.mean(-1, keepdims=True)
    var = jnp.square(y - mu).mean(-1, keepdims=True)
    y = ((y - mu) * lax.rsqrt(var + GN_EPS)).reshape(B, T, D_RWKV) * lnx_g + lnx_b
    bonus = jnp.sum(r4 * k4 * r_k, axis=-1, keepdims=True) * v4
    y = y + bonus.reshape(B, T, D_RWKV)
    return (y * g).astype(p_r.dtype)


def _moe(h, layer, router_w, router_b, w_gu, b_gu, w_down, b_down):
    N, D = h.shape
    logits = (h @ router_w[layer] + router_b[layer]).astype(jnp.float32)
    top_logit, top_idx = lax.top_k(logits, TOP_K)
    gates = jax.nn.softmax(top_logit, axis=-1)
    A = N * TOP_K
    flat_e = top_idx.reshape(A)
    flat_tok = jnp.repeat(jnp.arange(N, dtype=jnp.int32), TOP_K)
    flat_gate = gates.reshape(A)
    order = jnp.argsort(flat_e)
    se, st, sg = flat_e[order], flat_tok[order], flat_gate[order]
    counts = jnp.bincount(flat_e, length=N_EXPERTS)
    padded = (counts + MOE_BLOCK - 1) // MOE_BLOCK * MOE_BLOCK
    starts = jnp.cumsum(counts) - counts
    p_ends = jnp.cumsum(padded)
    p_starts = p_ends - padded
    dest = p_starts[se] + (jnp.arange(A) - starts[se])
    n_blocks = (A + N_EXPERTS * (MOE_BLOCK - 1) + MOE_BLOCK - 1) // MOE_BLOCK
    rows = n_blocks * MOE_BLOCK
    row_tok = jnp.zeros((rows,), jnp.int32).at[dest].set(st)
    block_e = jnp.minimum(jnp.searchsorted(p_ends, jnp.arange(n_blocks) * MOE_BLOCK, side='right'),
                          N_EXPERTS - 1)

    def run_block(args):
        tok, e = args
        xb = h[tok]
        gu = xb @ w_gu[layer, e] + b_gu[layer, e]
        gate = jnp.minimum(gu[:, :D_EXPERT], SWIGLU_LIMIT)
        up = jnp.clip(gu[:, D_EXPERT:], -SWIGLU_LIMIT, SWIGLU_LIMIT)
        glu = gate * jax.nn.sigmoid(gate * SWIGLU_ALPHA)
        return ((up + 1.0) * glu) @ w_down[layer, e] + b_down[layer, e]

    ys = lax.map(run_block, (row_tok.reshape(n_blocks, MOE_BLOCK), block_e)).reshape(rows, D)
    contrib = ys[dest] * sg[:, None].astype(ys.dtype)
    return jnp.zeros((N, D), ys.dtype).at[st].add(contrib)


def setup_inputs(seed: int = 0) -> dict:
    key = jax.random.key(seed)
    ks = jax.random.split(key, 24)
    L = DEPTH

    def nrm(k, shape, s):
        return jax.random.normal(k, shape, jnp.float32) * s

    x = nrm(ks[0], (BATCH, SEQ, D_MODEL), 1.0)
    col_scale = jnp.concatenate([
        jnp.ones((2 * D_ATT,), jnp.float32), jnp.full((D_ATT,), DEEPNORM_BETA, jnp.float32),
        jnp.ones((2 * D_RWKV,), jnp.float32), jnp.full((D_RWKV,), DEEPNORM_BETA, jnp.float32),
        jnp.ones((D_DECAY_LORA + D_AAA_LORA + D_GATE_LORA,), jnp.float32)])
    w_in = nrm(ks[1], (L, D_MODEL, D_IN), D_MODEL ** -0.5) * col_scale
    mu_shift = jax.random.uniform(ks[2], (L, D_RWKV_IN), jnp.float32)
    w0 = -6.0 + 5.0 * jax.random.uniform(ks[3], (L, D_RWKV), jnp.float32)
    w_up = nrm(ks[4], (L, D_DECAY_LORA, D_RWKV), 0.5 * D_DECAY_LORA ** -0.5)
    a0 = nrm(ks[5], (L, D_RWKV), 0.1)
    a_up = nrm(ks[6], (L, D_AAA_LORA, D_RWKV), D_AAA_LORA ** -0.5)
    g_up = nrm(ks[7], (L, D_GATE_LORA, D_RWKV), D_GATE_LORA ** -0.5)
    k_k = 0.85 + nrm(ks[8], (L, D_RWKV), 0.05)
    k_a = 1.0 + nrm(ks[9], (L, D_RWKV), 0.05)
    r_k = nrm(ks[10], (L, RWKV_HEADS, RWKV_HEAD_DIM), 0.1)
    lnx_g = 1.0 + nrm(ks[11], (L, D_RWKV), 0.05)
    lnx_b = nrm(ks[12], (L, D_RWKV), 0.02)
    w_out = nrm(ks[13], (L, D_MIX, D_MODEL), DEEPNORM_BETA * D_MIX ** -0.5)
    ln1_g = 1.0 + nrm(ks[14], (L, D_MODEL), 0.05)
    ln1_b = nrm(ks[15], (L, D_MODEL), 0.02)
    router_w = nrm(ks[16], (L, D_MODEL, N_EXPERTS), D_MODEL ** -0.5)
    router_b = nrm(ks[17], (L, N_EXPERTS), 0.01)
    w_gu = nrm(ks[18], (L, N_EXPERTS, D_MODEL, 2 * D_EXPERT), D_MODEL ** -0.5)
    b_gu = nrm(ks[19], (L, N_EXPERTS, 2 * D_EXPERT), 0.01)
    w_down = nrm(ks[20], (L, N_EXPERTS, D_EXPERT, D_MODEL), DEEPNORM_BETA * D_EXPERT ** -0.5)
    b_down = nrm(ks[21], (L, N_EXPERTS, D_MODEL), 0.01)
    ln2_g = 1.0 + nrm(ks[22], (L, D_MODEL), 0.05)
    ln2_b = nrm(ks[23], (L, D_MODEL), 0.02)
    return {'x': x, 'w_in': w_in, 'mu_shift': mu_shift, 'w0': w0, 'w_up': w_up, 'a0': a0,
            'a_up': a_up, 'g_up': g_up, 'k_k': k_k, 'k_a': k_a, 'r_k': r_k, 'lnx_g': lnx_g,
            'lnx_b': lnx_b, 'w_out': w_out, 'ln1_g': ln1_g, 'ln1_b': ln1_b, 'router_w': router_w,
            'router_b': router_b, 'w_gu': w_gu, 'b_gu': b_gu, 'w_down': w_down, 'b_down': b_down,
            'ln2_g': ln2_g, 'ln2_b': ln2_b}


def reference(x, w_in, mu_shift, w0, w_up, a0, a_up, g_up, k_k, k_a, r_k, lnx_g, lnx_b, w_out,
              ln1_g, ln1_b, router_w, router_b, w_gu, b_gu, w_down, b_down, ln2_g, ln2_b):
    B, S, D = x.shape
    c0 = 3 * D_RWKV
    for l in range(DEPTH):
        proj = x @ w_in[l]
        q = proj[..., :D_ATT].reshape(B, S, ATT_HEADS, ATT_HEAD_DIM).transpose(0, 2, 1, 3)
        k = proj[..., D_ATT:2 * D_ATT].reshape(B, S, ATT_HEADS, ATT_HEAD_DIM).transpose(0, 2, 1, 3)
        v = proj[..., 2 * D_ATT:3 * D_ATT].reshape(B, S, ATT_HEADS, ATT_HEAD_DIM).transpose(0, 2, 1, 3)
        attn = _moba_attention(_rope(q), _rope(k), v)
        rw = proj[..., 3 * D_ATT:]
        rw_prev = jnp.pad(rw[:, :-1], ((0, 0), (1, 0), (0, 0)))
        rw = rw + (rw_prev - rw) * mu_shift[l]
        p_r = rw[..., :D_RWKV]
        p_k = rw[..., D_RWKV:2 * D_RWKV]
        p_v = rw[..., 2 * D_RWKV:c0]
        p_w = rw[..., c0:c0 + D_DECAY_LORA]
        p_a = rw[..., c0 + D_DECAY_LORA:c0 + D_DECAY_LORA + D_AAA_LORA]
        p_g = rw[..., c0 + D_DECAY_LORA + D_AAA_LORA:]
        rwkv = _rwkv7_mixer(p_r, p_k, p_v, p_w, p_a, p_g, w0[l], w_up[l], a0[l], a_up[l], g_up[l],
                            k_k[l], k_a[l], r_k[l], lnx_g[l], lnx_b[l])
        mix = jnp.concatenate([attn, rwkv], axis=-1) @ w_out[l]
        x = _layer_norm(DEEPNORM_ALPHA * x + mix, ln1_g[l], ln1_b[l])
        ffn = _moe(x.reshape(B * S, D), l, router_w, router_b, w_gu, b_gu, w_down, b_down).reshape(B, S, D)
        x = _layer_norm(DEEPNORM_ALPHA * x + ffn, ln2_g[l], ln2_b[l])
    return x
```

```python
import functools

import jax
import jax.numpy as jnp
from jax import lax
from jax.experimental import pallas as pl
from jax.experimental.pallas import tpu as pltpu

F32 = jnp.float32
BF16 = jnp.bfloat16
I32 = jnp.int32
HIGHEST = lax.Precision.HIGHEST

LANES = 128
D_MODEL = 2048
D_ATT = 1024
ATT_HEAD_DIM = 128
ATT_HEADS = D_ATT // ATT_HEAD_DIM
ROPE_THETA = 10000.0
MOBA_BLOCK = 256
MOBA_TOPK = 3
D_RWKV = 1024
RWKV_HEAD_DIM = 64
RWKV_PAIRS = D_RWKV // LANES
D_DECAY_LORA = 64
D_AAA_LORA = 64
D_GATE_LORA = 160
D_LORA = D_DECAY_LORA + D_AAA_LORA + D_GATE_LORA
D_LORA_PAD = 384
D_MAIN = 3 * D_ATT + 3 * D_RWKV
N_EXPERTS = 32
TOP_K = 4
D_EXPERT = 2048
SWIGLU_LIMIT = 7.0
SWIGLU_ALPHA = 1.702
MOE_BLOCK = 128
LN_EPS = 1e-5
GN_EPS = 64e-5
NEG_INF = -1e30
PICKED = -3e38
DEPTH = 1
DEEPNORM_ALPHA = (2 * DEPTH) ** 0.25
RWKV_CHUNK = 64
VMEM_LIMIT = 56 * 2 ** 20


def _params(semantics):
    return pltpu.CompilerParams(dimension_semantics=semantics, vmem_limit_bytes=VMEM_LIMIT)


def _dot(a, b, precision=None):
    return jnp.dot(a, b, preferred_element_type=F32, precision=precision)


def _dot_nt(a, b, precision=None):
    return lax.dot_general(a, b, (((1,), (1,)), ((), ())), preferred_element_type=F32,
                           precision=precision)


def _sigmoid(x):
    return 1.0 / (1.0 + jnp.exp(-x))


def _layer_norm(h, g, b):
    mu = jnp.mean(h, axis=-1, keepdims=True)
    d = h - mu
    var = jnp.mean(d * d, axis=-1, keepdims=True)
    return d * lax.rsqrt(var + LN_EPS) * g + b


def _mm_kernel(x_ref, w_ref, o_ref):
    o_ref[...] = _dot(x_ref[...], w_ref[...]).astype(o_ref.dtype)


def _matmul(x, w, n_cols, tm, tn):
    m, k = x.shape
    return pl.pallas_call(
        _mm_kernel,
        grid=(n_cols // tn, m // tm),
        in_specs=[pl.BlockSpec((tm, k), lambda n, i: (i, 0)),
                  pl.BlockSpec((k, tn), lambda n, i: (0, n))],
        out_specs=pl.BlockSpec((tm, tn), lambda n, i: (i, n)),
        out_shape=jax.ShapeDtypeStruct((m, n_cols), F32),
        compiler_params=_params(("arbitrary", "arbitrary")),
        name="proj_matmul",
    )(x, w)


def _rope(t, cos, sin_signed):
    return t * cos + pltpu.roll(t, ATT_HEAD_DIM // 2, 1) * sin_signed


def _att_prep_kernel(k_ref, v_ref, cos_ref, sin_ref, kr_ref, km_ref, vt_ref):
    cos = cos_ref[...]
    sin = sin_ref[...]
    for h in range(ATT_HEADS):
        sl = slice(h * ATT_HEAD_DIM, (h + 1) * ATT_HEAD_DIM)
        kr = _rope(k_ref[:, sl], cos, sin)
        kr_ref[:, sl] = kr.astype(BF16)
        km_ref[0, :, sl] = jnp.mean(kr, axis=0, keepdims=True)
    vt_ref[0] = v_ref[...].T.astype(BF16)


def _att_prep(proj, cos, sin_signed, batch, seq):
    n = batch * seq
    nb = seq // MOBA_BLOCK
    return pl.pallas_call(
        _att_prep_kernel,
        grid=(n // MOBA_BLOCK,),
        in_specs=[pl.BlockSpec((MOBA_BLOCK, D_ATT), lambda i: (i, 1)),
                  pl.BlockSpec((MOBA_BLOCK, D_ATT), lambda i: (i, 2)),
                  pl.BlockSpec((MOBA_BLOCK, ATT_HEAD_DIM), lambda i: (i % nb, 0)),
                  pl.BlockSpec((MOBA_BLOCK, ATT_HEAD_DIM), lambda i: (i % nb, 0))],
        out_specs=[pl.BlockSpec((MOBA_BLOCK, D_ATT), lambda i: (i, 0)),
                   pl.BlockSpec((1, 1, D_ATT), lambda i: (i, 0, 0)),
                   pl.BlockSpec((1, D_ATT, MOBA_BLOCK), lambda i: (i, 0, 0))],
        out_shape=[jax.ShapeDtypeStruct((n, D_ATT), BF16),
                   jax.ShapeDtypeStruct((n // MOBA_BLOCK, 1, D_ATT), F32),
                   jax.ShapeDtypeStruct((n // MOBA_BLOCK, D_ATT, MOBA_BLOCK), BF16)],
        compiler_params=_params(("arbitrary",)),
        name="att_prep",
    )(proj, proj, cos, sin_signed)


def _att_kernel(q_ref, cos_ref, sin_ref, k_ref, vt_ref, km_ref, o_ref, sel_ref, *, nb):
    i = pl.program_id(2)
    blk = MOBA_BLOCK
    scale = ATT_HEAD_DIM ** -0.5
    qr = _rope(q_ref[...], cos_ref[...], sin_ref[...])
    qb = qr.astype(BF16)

    gate = _dot_nt(km_ref[0], qr, HIGHEST)
    row = lax.broadcasted_iota(I32, gate.shape, 0)
    valid = row < i
    g = jnp.where(valid, gate, NEG_INF)
    sel = jnp.zeros(gate.shape, F32)
    for _ in range(MOBA_TOPK):
        top = jnp.max(g, axis=0, keepdims=True)
        first = jnp.min(jnp.where((g == top) & valid, row, nb), axis=0, keepdims=True)
        pick = row == first
        sel = jnp.where(pick, 1.0, sel)
        g = jnp.where(pick, NEG_INF, g)
    sel_ref[...] = sel

    def scores(j):
        kj = k_ref[pl.ds(pl.multiple_of(j * blk, blk), blk), :]
        return _dot_nt(kj, qb) * scale

    kv_pos = lax.broadcasted_iota(I32, (blk, blk), 0)
    q_pos = lax.broadcasted_iota(I32, (blk, blk), 1)
    s = jnp.where(kv_pos <= q_pos, scores(i), NEG_INF)
    m = jnp.max(s, axis=0, keepdims=True)
    p = jnp.exp(s - m)
    l = jnp.sum(p, axis=0, keepdims=True)
    acc = _dot(vt_ref[0, i], p.astype(BF16))

    def body(j, carry):
        m, l, acc = carry
        s = jnp.where(sel_ref[pl.ds(j, 1), :] > 0.5, scores(j), NEG_INF)
        m_new = jnp.maximum(m, jnp.max(s, axis=0, keepdims=True))
        alpha = jnp.exp(m - m_new)
        p = jnp.exp(s - m_new)
        l = alpha * l + jnp.sum(p, axis=0, keepdims=True)
        acc = alpha * acc + _dot(vt_ref[0, j], p.astype(BF16))
        return m_new, l, acc

    m, l, acc = lax.fori_loop(0, i, body, (m, l, acc))
    o_ref[...] = (acc / l).T.astype(o_ref.dtype)


def _attention(proj, cos, sin_signed, k_r, kmean, v_t, batch, seq):
    nb = seq // MOBA_BLOCK
    n = batch * seq
    return pl.pallas_call(
        functools.partial(_att_kernel, nb=nb),
        grid=(batch, ATT_HEADS, nb),
        in_specs=[pl.BlockSpec((MOBA_BLOCK, ATT_HEAD_DIM), lambda b, h, i: (b * nb + i, h)),
                  pl.BlockSpec((MOBA_BLOCK, ATT_HEAD_DIM), lambda b, h, i: (i, 0)),
                  pl.BlockSpec((MOBA_BLOCK, ATT_HEAD_DIM), lambda b, h, i: (i, 0)),
                  pl.BlockSpec((seq, ATT_HEAD_DIM), lambda b, h, i: (b, h)),
                  pl.BlockSpec((1, nb, ATT_HEAD_DIM, MOBA_BLOCK), lambda b, h, i: (b, 0, h, 0)),
                  pl.BlockSpec((1, nb, ATT_HEAD_DIM), lambda b, h, i: (b, 0, h))],
        out_specs=pl.BlockSpec((MOBA_BLOCK, ATT_HEAD_DIM), lambda b, h, i: (b * nb + i, h)),
        out_shape=jax.ShapeDtypeStruct((n, D_ATT), BF16),
        scratch_shapes=[pltpu.VMEM((nb, MOBA_BLOCK), F32)],
        compiler_params=_params(("arbitrary", "arbitrary", "arbitrary")),
        name="moba_attention",
    )(proj, cos, sin_signed, k_r, v_t.reshape(batch, nb, D_ATT, MOBA_BLOCK),
      kmean.reshape(batch, nb, D_ATT))


def _head_sum(x, seg):
    parts = [_dot(x[:, p * LANES:(p + 1) * LANES], seg, HIGHEST) for p in range(RWKV_PAIRS)]
    return jnp.concatenate(parts, axis=1)


def _rwkv_kernel(main_ref, tail_ref, mum_ref, mut_ref, w0_ref, a0_ref, kkw_ref, ka_ref, rk_ref,
                 lng_ref, lnb_ref, wup_ref, aup_ref, gup_ref, o_ref, s_ref, pm_ref, pt_ref):
    c_idx = pl.program_id(1)
    ch = RWKV_CHUNK

    @pl.when(c_idx == 0)
    def _():
        s_ref[...] = jnp.zeros(s_ref.shape, F32)
        pm_ref[...] = jnp.zeros(pm_ref.shape, F32)
        pt_ref[...] = jnp.zeros(pt_ref.shape, F32)

    def token_shift(x, prev_ref, mu):
        row = lax.broadcasted_iota(I32, x.shape, 0)
        prev = jnp.where(row == 0, prev_ref[0:1, :], pltpu.roll(x, 1, 0))
        prev_ref[0:1, :] = x[ch - 1:ch, :]
        return x + (prev - x) * mu

    xm = token_shift(main_ref[...], pm_ref, mum_ref[...])
    xt = token_shift(tail_ref[...], pt_ref, mut_ref[...])
    p_r = xm[:, :D_RWKV]
    p_k = xm[:, D_RWKV:2 * D_RWKV]
    p_v = xm[:, 2 * D_RWKV:]

    z = -(w0_ref[...] + _dot(jnp.tanh(xt), wup_ref[...], HIGHEST))
    softplus = jnp.maximum(z, 0.0) + jnp.log(1.0 + jnp.exp(-jnp.abs(z)))
    ld = -jnp.exp(-softplus - 0.5)
    asig = _sigmoid(a0_ref[...] + _dot(xt, aup_ref[...], HIGHEST))
    gate = _dot(_sigmoid(xt), gup_ref[...], HIGHEST)

    r_i = lax.broadcasted_iota(I32, (LANES, LANES), 0)
    c_i = lax.broadcasted_iota(I32, (LANES, LANES), 1)
    same_head = (r_i >> 6) == (c_i >> 6)
    seg = same_head.astype(F32)
    kx = p_k * kkw_ref[...]
    kk = kx / jnp.maximum(jnp.sqrt(_head_sum(kx * kx, seg)), 1e-12)
    kmod = p_k * (1.0 + (asig - 1.0) * ka_ref[...])
    a_vec = -kk
    b_vec = kk * asig

    tri = (lax.broadcasted_iota(I32, (ch, ch), 1) <= lax.broadcasted_iota(I32, (ch, ch), 0)).astype(F32)
    cum = _dot(tri, ld, HIGHEST)
    cprev = cum - ld
    mid = cum[ch // 2 - 1:ch // 2, :]
    c_end = cum[ch - 1:ch, :]
    e_bk = jnp.exp(mid - cum)
    a_rel = a_vec * jnp.exp(cprev - mid)
    r_rel = p_r * jnp.exp(cum - mid)
    b_rel = b_vec * e_bk
    k_rel = kmod * e_bk
    a_abs = a_vec * jnp.exp(cprev)
    r_abs = p_r * jnp.exp(cum)
    e_end = jnp.exp(c_end - cum)
    b_end = b_vec * e_end
    k_end = kmod * e_end
    p_end = jnp.exp(c_end)

    first_head = lax.broadcasted_iota(I32, (ch, LANES), 1) < RWKV_HEAD_DIM
    strict = same_head & ((c_i & 63) < (r_i & 63))
    incl = same_head & ((c_i & 63) <= (r_i & 63))
    eye = (r_i == c_i).astype(F32)

    def stack(x):
        return jnp.concatenate([jnp.where(first_head, x, 0.0), jnp.where(first_head, 0.0, x)], axis=0)

    def dup(x):
        return jnp.concatenate([x, x], axis=0)

    ys = []
    for p in range(RWKV_PAIRS):
        sl = slice(p * LANES, (p + 1) * LANES)
        lhs = jnp.concatenate([stack(a_rel[:, sl]), stack(r_rel[:, sl])], axis=0).astype(BF16)
        rhs = jnp.concatenate([dup(b_rel[:, sl]), dup(k_rel[:, sl])], axis=0).astype(BF16)
        gram = _dot_nt(lhs, rhs)
        l_ab = jnp.where(strict, gram[:LANES, :LANES], 0.0)
        l_ak = jnp.where(strict, gram[:LANES, LANES:], 0.0)
        m_rb = jnp.where(incl, gram[LANES:, :LANES], 0.0)
        m_rk = jnp.where(incl, gram[LANES:, LANES:], 0.0)

        t_inv = eye + l_ab
        pw = l_ab
        for _ in range(5):
            pw_b = pw.astype(BF16)
            pw = _dot(pw_b, pw_b)
            t_inv = t_inv + _dot(t_inv.astype(BF16), pw.astype(BF16))

        s_old = s_ref[p]
        s_b = s_old.astype(BF16)
        v_s = stack(p_v[:, sl]).astype(BF16)
        x1 = _dot_nt(stack(a_abs[:, sl]).astype(BF16), s_b) + _dot(l_ak.astype(BF16), v_s)
        u = _dot(t_inv.astype(BF16), x1.astype(BF16))
        u_b = u.astype(BF16)
        y_s = (_dot_nt(stack(r_abs[:, sl]).astype(BF16), s_b)
               + _dot(jnp.concatenate([m_rb, m_rk], axis=1).astype(BF16),
                      jnp.concatenate([u_b, v_s], axis=0)))
        ys.append(y_s[:ch] + y_s[ch:])
        uv_t = jnp.concatenate([u.T, stack(p_v[:, sl]).T], axis=1).astype(BF16)
        bk = jnp.concatenate([stack(b_end[:, sl]), stack(k_end[:, sl])], axis=0).astype(BF16)
        s_ref[p] = s_old * p_end[:, sl] + _dot(uv_t, bk)

    y = jnp.concatenate(ys, axis=1)
    inv_n = 1.0 / RWKV_HEAD_DIM
    mu = _head_sum(y, seg) * inv_n
    d = y - mu
    var = _head_sum(d * d, seg) * inv_n
    y = d * lax.rsqrt(var + GN_EPS) * lng_ref[...] + lnb_ref[...]
    y = y + _head_sum(p_r * kmod * rk_ref[...], seg) * p_v
    o_ref[...] = (y * gate).astype(o_ref.dtype)


def _rwkv(proj, proj_tail, vecs, wup, aup, gup, batch, seq):
    n = batch * seq
    ch = RWKV_CHUNK
    nc = seq // ch
    row = lambda b, c: (b * nc + c, 0)
    const = lambda b, c: (0, 0)
    vec_specs = [pl.BlockSpec(v.shape, const) for v in vecs]
    return pl.pallas_call(
        _rwkv_kernel,
        grid=(batch, nc),
        in_specs=[pl.BlockSpec((ch, 3 * D_RWKV), lambda b, c: (b * nc + c, 1)),
                  pl.BlockSpec((ch, D_LORA_PAD), row)] + vec_specs
                 + [pl.BlockSpec((D_LORA_PAD, D_RWKV), const)] * 3,
        out_specs=pl.BlockSpec((ch, D_RWKV), row),
        out_shape=jax.ShapeDtypeStruct((n, D_RWKV), BF16),
        scratch_shapes=[pltpu.VMEM((RWKV_PAIRS, LANES, LANES), F32),
                        pltpu.VMEM((8, 3 * D_RWKV), F32),
                        pltpu.VMEM((8, D_LORA_PAD), F32)],
        compiler_params=_params(("arbitrary", "arbitrary")),
        name="rwkv7_mixer",
    )(proj, proj_tail, *vecs, wup, aup, gup)


def _outproj_kernel(x_ref, att_ref, rw_ref, wo_ref, g_ref, b_ref, rw_w_ref, rw_b_ref,
                    x1_ref, idx_ref, gate_ref):
    mix = _dot(att_ref[...], wo_ref[:D_ATT, :]) + _dot(rw_ref[...], wo_ref[D_ATT:, :])
    x1 = _layer_norm(DEEPNORM_ALPHA * x_ref[...] + mix, g_ref[...], b_ref[...])
    x1_ref[...] = x1

    logits = _dot(x1, rw_w_ref[...], HIGHEST) + rw_b_ref[...]
    lane = lax.broadcasted_iota(I32, logits.shape, 1)
    idx_out = jnp.zeros(logits.shape, I32)
    val_out = jnp.zeros(logits.shape, F32)
    top0 = None
    for k in range(TOP_K):
        top = jnp.max(logits, axis=1, keepdims=True)
        first = jnp.min(jnp.where(logits == top, lane, LANES), axis=1, keepdims=True)
        if top0 is None:
            top0 = top
        idx_out = jnp.where(lane == k, first, idx_out)
        val_out = jnp.where(lane == k, jnp.exp(top - top0), val_out)
        logits = jnp.where(lane == first, PICKED, logits)
    idx_ref[...] = idx_out
    gate_ref[...] = val_out / jnp.sum(val_out, axis=1, keepdims=True)


def _outproj(x2, attn, rwkv, w_out_b, ln_g, ln_b, router_w_pad, router_b_pad):
    n = x2.shape[0]
    tm = 256
    row = lambda i: (i, 0)
    const = lambda i: (0, 0)
    return pl.pallas_call(
        _outproj_kernel,
        grid=(n // tm,),
        in_specs=[pl.BlockSpec((tm, D_MODEL), row),
                  pl.BlockSpec((tm, D_ATT), row),
                  pl.BlockSpec((tm, D_RWKV), row),
                  pl.BlockSpec((D_MODEL, D_MODEL), const),
                  pl.BlockSpec((1, D_MODEL), const),
                  pl.BlockSpec((1, D_MODEL), const),
                  pl.BlockSpec((D_MODEL, LANES), const),
                  pl.BlockSpec((1, LANES), const)],
        out_specs=[pl.BlockSpec((tm, D_MODEL), row),
                   pl.BlockSpec((tm, LANES), row),
                   pl.BlockSpec((tm, LANES), row)],
        out_shape=[jax.ShapeDtypeStruct((n, D_MODEL), F32),
                   jax.ShapeDtypeStruct((n, LANES), I32),
                   jax.ShapeDtypeStruct((n, LANES), F32)],
        compiler_params=_params(("arbitrary",)),
        name="outproj_ln_router",
    )(x2, attn, rwkv, w_out_b, ln_g, ln_b, router_w_pad, router_b_pad)


def _row_copy(src_hbm, row, dst, dst_row, sem):
    return pltpu.make_async_copy(src_hbm.at[pl.ds(row, 1), :], dst.at[dst_row], sem)


def _gather_kernel(idx_ref, x_hbm, o_ref, buf_ref, sem):
    def start(r, carry):
        _row_copy(x_hbm, idx_ref[0, 0, r], buf_ref, r, sem).start()
        return carry
    lax.fori_loop(0, MOE_BLOCK, start, 0)

    def wait(r, carry):
        _row_copy(x_hbm, 0, buf_ref, r, sem).wait()
        return carry
    lax.fori_loop(0, MOE_BLOCK, wait, 0)
    o_ref[...] = buf_ref[...].reshape(MOE_BLOCK, D_MODEL).astype(o_ref.dtype)


def _gather_rows(x1, row_tok, n_blocks):
    return pl.pallas_call(
        _gather_kernel,
        grid=(n_blocks,),
        in_specs=[pl.BlockSpec((1, 1, MOE_BLOCK), lambda i: (i, 0, 0), memory_space=pltpu.SMEM),
                  pl.BlockSpec(memory_space=pl.ANY)],
        out_specs=pl.BlockSpec((MOE_BLOCK, D_MODEL), lambda i: (i, 0)),
        out_shape=jax.ShapeDtypeStruct((n_blocks * MOE_BLOCK, D_MODEL), BF16),
        scratch_shapes=[pltpu.VMEM((MOE_BLOCK, 1, D_MODEL), F32), pltpu.SemaphoreType.DMA],
        compiler_params=_params(("arbitrary",)),
        name="moe_gather",
    )(row_tok.reshape(n_blocks, 1, MOE_BLOCK), x1)


def _expert_changed(be_ref, rb):
    prev = be_ref[jnp.maximum(rb - 1, 0)]
    return (rb == 0) | (be_ref[rb] != prev)


def _gateup_kernel(be_ref, x_ref, wg_ref, wu_ref, bg_ref, bu_ref, o_ref, wgb_ref, wub_ref):
    rb = pl.program_id(1)

    @pl.when(_expert_changed(be_ref, rb))
    def _():
        wgb_ref[...] = wg_ref[0].astype(BF16)
        wub_ref[...] = wu_ref[0].astype(BF16)

    x = x_ref[...]
    gate = jnp.minimum(_dot(x, wgb_ref[...]) + bg_ref[0], SWIGLU_LIMIT)
    up = jnp.clip(_dot(x, wub_ref[...]) + bu_ref[0], -SWIGLU_LIMIT, SWIGLU_LIMIT)
    glu = gate * _sigmoid(gate * SWIGLU_ALPHA)
    o_ref[...] = ((up + 1.0) * glu).astype(o_ref.dtype)


def _gateup(block_e, xs, w_gu, b_gu, n_blocks):
    tn = 512
    nt = D_EXPERT // tn
    grid_spec = pltpu.PrefetchScalarGridSpec(
        num_scalar_prefetch=1,
        grid=(nt, n_blocks),
        in_specs=[pl.BlockSpec((MOE_BLOCK, D_MODEL), lambda t, r, be: (r, 0)),
                  pl.BlockSpec((1, D_MODEL, tn), lambda t, r, be: (be[r], 0, t)),
                  pl.BlockSpec((1, D_MODEL, tn), lambda t, r, be: (be[r], 0, nt + t)),
                  pl.BlockSpec((1, 1, tn), lambda t, r, be: (be[r], 0, t)),
                  pl.BlockSpec((1, 1, tn), lambda t, r, be: (be[r], 0, nt + t))],
        out_specs=pl.BlockSpec((MOE_BLOCK, tn), lambda t, r, be: (r, t)),
        scratch_shapes=[pltpu.VMEM((D_MODEL, tn), BF16), pltpu.VMEM((D_MODEL, tn), BF16)],
    )
    return pl.pallas_call(
        _gateup_kernel,
        grid_spec=grid_spec,
        out_shape=jax.ShapeDtypeStruct((n_blocks * MOE_BLOCK, D_EXPERT), BF16),
        compiler_params=_params(("arbitrary", "arbitrary")),
        name="moe_gate_up",
    )(block_e, xs, w_gu, w_gu, b_gu, b_gu)


def _down_kernel(be_ref, h_ref, w_ref, b_ref, o_ref, wb_ref):
    rb = pl.program_id(1)

    @pl.when(_expert_changed(be_ref, rb))
    def _():
        wb_ref[...] = w_ref[0].astype(BF16)

    o_ref[...] = _dot(h_ref[...], wb_ref[...]) + b_ref[0]


def _down(block_e, hs, w_down, b_down, n_blocks):
    tn = 1024
    nt = D_MODEL // tn
    grid_spec = pltpu.PrefetchScalarGridSpec(
        num_scalar_prefetch=1,
        grid=(nt, n_blocks),
        in_specs=[pl.BlockSpec((MOE_BLOCK, D_EXPERT), lambda t, r, be: (r, 0)),
                  pl.BlockSpec((1, D_EXPERT, tn), lambda t, r, be: (be[r], 0, t)),
                  pl.BlockSpec((1, 1, tn), lambda t, r, be: (be[r], 0, t))],
        out_specs=pl.BlockSpec((MOE_BLOCK, tn), lambda t, r, be: (r, t)),
        scratch_shapes=[pltpu.VMEM((D_EXPERT, tn), BF16)],
    )
    return pl.pallas_call(
        _down_kernel,
        grid_spec=grid_spec,
        out_shape=jax.ShapeDtypeStruct((n_blocks * MOE_BLOCK, D_MODEL), F32),
        compiler_params=_params(("arbitrary", "arbitrary")),
        name="moe_down",
    )(block_e, hs, w_down, b_down)


def _combine_kernel(rows_ref, y_hbm, x1_ref, gate_ref, g_ref, b_ref, o_ref, buf_ref, sem):
    def start(t, carry):
        for k in range(TOP_K):
            _row_copy(y_hbm, rows_ref[0, 0, t * TOP_K + k], buf_ref.at[k], t, sem).start()
        return carry
    lax.fori_loop(0, MOE_BLOCK, start, 0)

    def wait(t, carry):
        for k in range(TOP_K):
            _row_copy(y_hbm, 0, buf_ref.at[k], t, sem).wait()
        return carry
    lax.fori_loop(0, MOE_BLOCK, wait, 0)

    gates = gate_ref[...]
    ffn = gates[:, 0:1] * buf_ref[0].reshape(MOE_BLOCK, D_MODEL)
    for k in range(1, TOP_K):
        ffn = ffn + gates[:, k:k + 1] * buf_ref[k].reshape(MOE_BLOCK, D_MODEL)
    o_ref[...] = _layer_norm(DEEPNORM_ALPHA * x1_ref[...] + ffn, g_ref[...], b_ref[...])


def _combine(tok_rows, ys, x1, gates, ln_g, ln_b):
    n = x1.shape[0]
    tm = MOE_BLOCK
    row = lambda i: (i, 0)
    const = lambda i: (0, 0)
    return pl.pallas_call(
        _combine_kernel,
        grid=(n // tm,),
        in_specs=[pl.BlockSpec((1, 1, tm * TOP_K), lambda i: (i, 0, 0), memory_space=pltpu.SMEM),
                  pl.BlockSpec(memory_space=pl.ANY),
                  pl.BlockSpec((tm, D_MODEL), row),
                  pl.BlockSpec((tm, LANES), row),
                  pl.BlockSpec((1, D_MODEL), const),
                  pl.BlockSpec((1, D_MODEL), const)],
        out_specs=pl.BlockSpec((tm, D_MODEL), row),
        out_shape=jax.ShapeDtypeStruct((n, D_MODEL), F32),
        scratch_shapes=[pltpu.VMEM((TOP_K, tm, 1, D_MODEL), F32), pltpu.SemaphoreType.DMA],
        compiler_params=_params(("arbitrary",)),
        name="moe_combine_ln",
    )(tok_rows.reshape(n // tm, 1, tm * TOP_K), ys, x1, gates, ln_g, ln_b)


def _routing_tables(top_idx, n_tok):
    a = n_tok * TOP_K
    flat_e = top_idx.reshape(a)
    order = jnp.argsort(flat_e)
    se = flat_e[order]
    st = (order // TOP_K).astype(I32)
    counts = jnp.bincount(flat_e, length=N_EXPERTS)
    padded = (counts + MOE_BLOCK - 1) // MOE_BLOCK * MOE_BLOCK
    starts = jnp.cumsum(counts) - counts
    p_ends = jnp.cumsum(padded)
    p_starts = p_ends - padded
    dest = (p_starts[se] + (jnp.arange(a) - starts[se])).astype(I32)
    n_blocks = (a + N_EXPERTS * (MOE_BLOCK - 1) + MOE_BLOCK - 1) // MOE_BLOCK
    row_tok = jnp.zeros((n_blocks * MOE_BLOCK,), I32).at[dest].set(st)
    block_e = jnp.minimum(jnp.searchsorted(p_ends, jnp.arange(n_blocks) * MOE_BLOCK, side='right'),
                          N_EXPERTS - 1).astype(I32)
    tok_rows = jnp.zeros((a,), I32).at[order].set(dest)
    return row_tok, block_e, tok_rows, n_blocks


def _pad_rows(w, lo, total):
    return jnp.zeros((total, w.shape[1]), w.dtype).at[lo:lo + w.shape[0]].set(w)


def kernel(x, w_in, mu_shift, w0, w_up, a0, a_up, g_up, k_k, k_a, r_k, lnx_g, lnx_b, w_out,
           ln1_g, ln1_b, router_w, router_b, w_gu, b_gu, w_down, b_down, ln2_g, ln2_b):
    batch, seq, d = x.shape
    n = batch * seq
    x2 = x.reshape(n, d)
    for l in range(DEPTH):
        w_in_b = w_in[l].astype(BF16)
        w_tail = jnp.zeros((d, D_LORA_PAD), BF16).at[:, :D_LORA].set(w_in_b[:, D_MAIN:])
        mu_main = mu_shift[l][None, :3 * D_RWKV]
        mu_tail = jnp.zeros((1, D_LORA_PAD), F32).at[0, :D_LORA].set(mu_shift[l][3 * D_RWKV:])
        wup = _pad_rows(w_up[l], 0, D_LORA_PAD)
        aup = _pad_rows(a_up[l], D_DECAY_LORA, D_LORA_PAD)
        gup = _pad_rows(g_up[l], D_DECAY_LORA + D_AAA_LORA, D_LORA_PAD)
        vecs = [mu_main, mu_tail, w0[l][None], a0[l][None], k_k[l][None], k_a[l][None],
                r_k[l].reshape(1, D_RWKV), lnx_g[l][None], lnx_b[l][None]]
        inv = ROPE_THETA ** (-jnp.arange(0, ATT_HEAD_DIM, 2, dtype=F32) / ATT_HEAD_DIM)
        ang = jnp.arange(seq, dtype=F32)[:, None] * inv[None, :]
        cos = jnp.concatenate([jnp.cos(ang), jnp.cos(ang)], axis=-1)
        sin_signed = jnp.concatenate([-jnp.sin(ang), jnp.sin(ang)], axis=-1)
        router_w_pad = jnp.zeros((d, LANES), F32).at[:, :N_EXPERTS].set(router_w[l])
        router_b_pad = jnp.full((1, LANES), NEG_INF, F32).at[0, :N_EXPERTS].set(router_b[l])

        xb = x2.astype(BF16)
        proj = _matmul(xb, w_in_b, D_MAIN, 512, 1024)
        proj_tail = _matmul(xb, w_tail, D_LORA_PAD, 512, D_LORA_PAD)

        k_r, kmean, v_t = _att_prep(proj, cos, sin_signed, batch, seq)
        attn = _attention(proj, cos, sin_signed, k_r, kmean, v_t, batch, seq)
        rwkv = _rwkv(proj, proj_tail, vecs, wup, aup, gup, batch, seq)

        x1, top_idx, gates = _outproj(x2, attn, rwkv, w_out[l].astype(BF16), ln1_g[l][None],
                                           ln1_b[l][None], router_w_pad, router_b_pad)

        row_tok, block_e, tok_rows, n_blocks = _routing_tables(top_idx[:, :TOP_K], n)
        xs = _gather_rows(x1, row_tok, n_blocks)
        hs = _gateup(block_e, xs, w_gu[l], b_gu[l].reshape(N_EXPERTS, 1, 2 * D_EXPERT), n_blocks)
        ys = _down(block_e, hs, w_down[l], b_down[l].reshape(N_EXPERTS, 1, D_MODEL), n_blocks)
        x2 = _combine(tok_rows, ys, x1, gates, ln2_g[l][None], ln2_b[l][None])
    return x2.reshape(batch, seq, d)
```
